```python
import math
import jax
import jax.numpy as jnp
from jax import lax
import numpy as np

D_MODEL = 1024
BATCH = 8
SEQ = 4096
DEPTH = 4

GRID_W = 64
CTX_LEN = 256
N_MIXERS = 4
MIX_WIDTH = D_MODEL
GROUP_W = MIX_WIDTH // N_MIXERS
D_FF = 4 * D_MODEL
CONV_K = 5
CHUNK = 64
Q_BLOCK = 128
EPS = 1e-6
ROPE_THETA = 10000.0

GDN_HEADS = 4
GDN_HEAD_DIM = GROUP_W // GDN_HEADS
GDN_IN = 4 * GROUP_W + 4 * GDN_HEADS
S5_GROUP = 16
S5_GROUPS = GROUP_W // S5_GROUP
S5_STATE = 64
S5_IN = GROUP_W
SSD_HEAD_DIM = 64
SSD_HEADS = GROUP_W // SSD_HEAD_DIM
SSD_GROUPS = 2
SSD_STATE = 128
SSD_CONV_CH = GROUP_W + 2 * SSD_GROUPS * SSD_STATE
SSD_IN = GROUP_W + SSD_CONV_CH + 2 * SSD_HEADS
MLA_HEADS = 4
MLA_NOPE = 64
MLA_ROPE = 32
MLA_QK = MLA_NOPE + MLA_ROPE
MLA_V = GROUP_W // MLA_HEADS
MLA_Q_RANK = 256
MLA_KV_RANK = 128
MLA_IN = MLA_Q_RANK + MLA_KV_RANK + MLA_ROPE

IN_WIDTHS = (GDN_IN, S5_IN, SSD_IN, MLA_IN)
IN_WIDTH = GDN_IN + S5_IN + SSD_IN + MLA_IN
F32 = jnp.float32

kernel_name = 'hybrid_parallel_heads_dit'


def _split(t, widths):
    offs = np.cumsum(widths)[:-1].tolist()
    return jnp.split(t, offs, axis=-1)


def _rms(x, g):
    xf = x.astype(F32)
    y = xf * lax.rsqrt(jnp.mean(jnp.square(xf), axis=-1, keepdims=True) + EPS)
    return (y * g.astype(F32)).astype(x.dtype)


def _l2norm(x):
    return x * lax.rsqrt(jnp.sum(jnp.square(x), axis=-1, keepdims=True) + EPS)


def _modulate(h, g, shift, scale):
    return _rms(h, g) * (1.0 + scale) + shift


def _dwconv(x, w):
    k, ch = w.shape
    return lax.conv_general_dilated(x, w[:, None, :].astype(x.dtype), window_strides=(1,),
                                    padding=[(k // 2, k // 2)],
                                    dimension_numbers=('NWC', 'WIO', 'NWC'),
                                    feature_group_count=ch)


def _bidirectional(scan_fn, ctx_inputs, lat_inputs, state0):
    y_ctx, y_lat = [], []
    for direction in range(2):
        rev = (lambda t: jnp.flip(t, axis=1)) if direction else (lambda t: t)
        oc, s_ctx = scan_fn(direction, tuple(rev(t) for t in ctx_inputs), state0)
        ol, _ = scan_fn(direction, tuple(rev(t) for t in lat_inputs), s_ctx)
        y_ctx.append(rev(oc))
        y_lat.append(rev(ol))
    return y_ctx[0] + y_ctx[1], y_lat[0] + y_lat[1]


def _to_chunks(t):
    b, tlen, h = t.shape[:3]
    t = t.reshape((b, tlen // CHUNK, CHUNK, h) + t.shape[3:])
    return jnp.moveaxis(t, 3, 1)


def gated_delta_rule(q, k, v, g, beta, s0):
    bsz, tlen, nh, dk = q.shape
    qc = _to_chunks(q * dk ** -0.5)
    kc = _to_chunks(k)
    vc = _to_chunks(v)
    gc = jnp.cumsum(_to_chunks(g), axis=-1)
    bc = _to_chunks(beta)[..., None]
    tri = jnp.tril(jnp.ones((CHUNK, CHUNK), dtype=bool))
    strict = jnp.tril(jnp.ones((CHUNK, CHUNK), dtype=bool), -1)
    decay = jnp.exp(jnp.where(tri, gc[..., :, None] - gc[..., None, :], -jnp.inf))
    kb = kc * bc
    a = jnp.where(strict, jnp.einsum('bhncd,bhnsd->bhncs', kb, kc) * decay, 0.0)
    eye = jnp.eye(CHUNK, dtype=a.dtype)
    t_inv = lax.linalg.triangular_solve(a + eye, jnp.broadcast_to(eye, a.shape), left_side=True,
                                        lower=True, unit_diagonal=True)
    u = t_inv @ (vc * bc)
    w = t_inv @ (kb * jnp.exp(gc)[..., None])
    qk = jnp.where(tri, jnp.einsum('bhncd,bhnsd->bhncs', qc, kc) * decay, 0.0)

    def step(s, xs):
        q_i, k_i, u_i, w_i, g_i, qk_i = xs
        v_new = u_i - w_i @ s
        o = (q_i * jnp.exp(g_i)[..., None]) @ s + qk_i @ v_new
        g_last = g_i[..., -1:]
        s = s * jnp.exp(g_last)[..., None] + jnp.einsum(
            'bhcd,bhce->bhde', k_i * jnp.exp(g_last - g_i)[..., None], v_new)
        return s, o

    xs = tuple(jnp.moveaxis(t, 2, 0) for t in (qc, kc, u, w, gc, qk))
    s_fin, o = lax.scan(step, s0, xs)
    o = jnp.moveaxis(o, 0, 2).reshape(bsz, nh, tlen, -1)
    return jnp.swapaxes(o, 1, 2), s_fin


def gdn_mixer(p_ctx, p_lat, conv_w, a_log, dt_bias, norm_g):
    dtype = p_lat.dtype

    def prep(p):
        bsz, tlen, _ = p.shape
        qkv, gate, a, b = _split(p, (3 * GROUP_W, GROUP_W, 2 * GDN_HEADS, 2 * GDN_HEADS))
        qkv = jax.nn.silu(_dwconv(qkv, conv_w)).astype(F32)
        q, k, v = [t.reshape(bsz, tlen, GDN_HEADS, GDN_HEAD_DIM) for t in jnp.split(qkv, 3, axis=-1)]
        a = a.astype(F32).reshape(bsz, tlen, 2, GDN_HEADS)
        g = -jnp.exp(a_log.astype(F32)) * jax.nn.softplus(a + dt_bias.astype(F32))
        beta = jax.nn.sigmoid(b.astype(F32).reshape(bsz, tlen, 2, GDN_HEADS))
        return (_l2norm(q), _l2norm(k), v, g, beta), gate

    def scan_fn(direction, inputs, s0):
        q, k, v, g, beta = inputs
        return gated_delta_rule(q, k, v, g[:, :, direction], beta[:, :, direction], s0)

    in_ctx, gate_ctx = prep(p_ctx)
    in_lat, gate_lat = prep(p_lat)
    s0 = jnp.zeros((p_ctx.shape[0], GDN_HEADS, GDN_HEAD_DIM, GDN_HEAD_DIM), F32)
    o_ctx, o_lat = _bidirectional(scan_fn, in_ctx, in_lat, s0)

    def out(o, gate):
        bsz, tlen = gate.shape[:2]
        gate = jax.nn.silu(gate.astype(F32)).reshape(bsz, tlen, GDN_HEADS, GDN_HEAD_DIM)
        return (_rms(o, norm_g) * gate).reshape(bsz, tlen, GROUP_W).astype(dtype)

    return out(o_ctx, gate_ctx), out(o_lat, gate_lat)


def _linear_combine(e1, e2):
    a1, b1 = e1
    a2, b2 = e2
    return a1 * a2, a2 * b1 + b2


def s5_mixer(u_ctx, u_lat, a_re, a_im, log_step, b_re, b_im, c_re, c_im, d_skip, w_glu, b_glu):
    dtype = u_lat.dtype
    lam = lax.complex(a_re.astype(F32), a_im.astype(F32))
    lam_bar = jnp.exp(lam * jnp.exp(log_step.astype(F32))[..., None])
    b_bar = ((lam_bar - 1.0) / lam)[..., None] * lax.complex(b_re.astype(F32), b_im.astype(F32))
    c_mat = lax.complex(c_re.astype(F32), c_im.astype(F32))

    def scan_fn(direction, inputs, h0):
        (u,) = inputs
        bsz, tlen, _ = u.shape
        ug = u.astype(F32).reshape(bsz, tlen, S5_GROUPS, S5_GROUP).astype(jnp.complex64)
        lb = lam_bar[direction]
        bu = jnp.einsum('gnc,btgc->btgn', b_bar[direction], ug)
        bu = bu.at[:, 0].add(lb * h0)
        _, h = lax.associative_scan(_linear_combine, (jnp.broadcast_to(lb, bu.shape), bu), axis=1)
        y = jnp.einsum('gcn,btgn->btgc', c_mat[direction], h).real
        return y.reshape(bsz, tlen, GROUP_W), h[:, -1]

    h0 = jnp.zeros((u_ctx.shape[0], S5_GROUPS, S5_STATE), jnp.complex64)
    y_ctx, y_lat = _bidirectional(scan_fn, (u_ctx,), (u_lat,), h0)

    def out(y, u):
        y = jax.nn.gelu(y + d_skip.astype(F32) * u.astype(F32))
        return (y * jax.nn.sigmoid(y @ w_glu.astype(F32) + b_glu.astype(F32))).astype(dtype)

    return out(y_ctx, u_ctx), out(y_lat, u_lat)


def ssd_scan(x, dt, a, b, c, h0):
    bsz, tlen, nh, hp = x.shape
    ng, ms = b.shape[2], b.shape[3]
    r = nh // ng
    n = tlen // CHUNK
    xc = (x * dt[..., None]).reshape(bsz, n, CHUNK, ng, r, hp)
    bc = b.reshape(bsz, n, CHUNK, ng, ms)
    cc = c.reshape(bsz, n, CHUNK, ng, ms)
    la = (dt * a).reshape(bsz, n, CHUNK, ng, r)
    cum = jnp.cumsum(jnp.transpose(la, (0, 3, 4, 1, 2)), axis=-1)
    tri = jnp.tril(jnp.ones((CHUNK, CHUNK), dtype=bool))
    lmat = jnp.exp(jnp.where(tri, cum[..., :, None] - cum[..., None, :], -jnp.inf))
    cb = jnp.einsum('bnlgm,bnsgm->bgnls', cc, bc)
    y_diag = jnp.einsum('bgnls,bgrnls,bnsgrp->bnlgrp', cb, lmat, xc)
    chunk_states = jnp.einsum('bnsgm,bgrns,bnsgrp->nbgrpm', bc, jnp.exp(cum[..., -1:] - cum), xc)
    chunk_decay = jnp.moveaxis(jnp.exp(cum[..., -1]), -1, 0)

    def step(h, xs):
        st, dec = xs
        return h * dec[..., None, None] + st, h

    h_fin, h_prev = lax.scan(step, h0.reshape(bsz, ng, r, hp, ms), (chunk_states, chunk_decay))
    y_off = jnp.einsum('bnlgm,nbgrpm,bgrnl->bnlgrp', cc, h_prev, jnp.exp(cum))
    y = (y_diag + y_off).reshape(bsz, tlen, nh, hp)
    return y, h_fin.reshape(bsz, nh, hp, ms)


def mamba2_mixer(p_ctx, p_lat, conv_w, conv_b, a_log, dt_bias, d_skip, norm_g):
    dtype = p_lat.dtype

    def prep(p):
        bsz, tlen, _ = p.shape
        z, xbc, dt = _split(p, (GROUP_W, SSD_CONV_CH, 2 * SSD_HEADS))
        xbc = jax.nn.silu(_dwconv(xbc, conv_w) + conv_b).astype(F32)
        xs, bs, cs = _split(xbc, (GROUP_W, SSD_GROUPS * SSD_STATE, SSD_GROUPS * SSD_STATE))
        xs = xs.reshape(bsz, tlen, SSD_HEADS, SSD_HEAD_DIM)
        bs = bs.reshape(bsz, tlen, SSD_GROUPS, SSD_STATE)
        cs = cs.reshape(bsz, tlen, SSD_GROUPS, SSD_STATE)
        dt = jax.nn.softplus(dt.astype(F32).reshape(bsz, tlen, 2, SSD_HEADS) + dt_bias.astype(F32))
        return (xs, bs, cs, dt), z

    a = -jnp.exp(a_log.astype(F32))

    def scan_fn(direction, inputs, h0):
        xs, bs, cs, dt = inputs
        return ssd_scan(xs, dt[:, :, direction], a[direction], bs, cs, h0)

    in_ctx, z_ctx = prep(p_ctx)
    in_lat, z_lat = prep(p_lat)
    h0 = jnp.zeros((p_ctx.shape[0], SSD_HEADS, SSD_HEAD_DIM, SSD_STATE), F32)
    y_ctx, y_lat = _bidirectional(scan_fn, in_ctx, in_lat, h0)

    def out(y, xs, z):
        bsz, tlen = z.shape[:2]
        y = (y + d_skip.astype(F32)[:, None] * xs).reshape(bsz, tlen, GROUP_W)
        return _rms(y * jax.nn.silu(z.astype(F32)), norm_g).astype(dtype)

    return out(y_ctx, in_ctx[0], z_ctx), out(y_lat, in_lat[0], z_lat)


def _axial_rope_angles(rows):
    pos = jnp.arange(rows * GRID_W)
    row = (pos // GRID_W).astype(F32)
    col = (pos % GRID_W).astype(F32)
    n_freq = MLA_ROPE // 4
    inv_freq = ROPE_THETA ** (-jnp.arange(n_freq, dtype=F32) / n_freq)
    ang = jnp.concatenate([row[:, None] * inv_freq, col[:, None] * inv_freq], axis=-1)
    return jnp.cos(ang), jnp.sin(ang)


def _rope(x, cos, sin):
    xp = x.reshape(x.shape[:-1] + (-1, 2))
    x0, x1 = xp[..., 0], xp[..., 1]
    cos, sin = cos[:, None, :], sin[:, None, :]
    return jnp.stack([x0 * cos - x1 * sin, x0 * sin + x1 * cos], axis=-1).reshape(x.shape)


def _attend(q, k, v):
    s = jnp.einsum('bqhd,bkhd->bhqk', q, k) * (MLA_QK ** -0.5)
    p = jax.nn.softmax(s.astype(F32), axis=-1)
    return jnp.einsum('bhqk,bkhd->bqhd', p, v.astype(F32))


def mla_mixer(p_ctx, p_lat, q_norm_g, kv_norm_g, w_uq, w_ukv, q_gain, k_gain, cos, sin):
    dtype = p_lat.dtype

    def qk_norm(t, gain):
        return jnp.concatenate([_rms(t[..., :MLA_NOPE], gain[:MLA_NOPE]),
                                _rms(t[..., MLA_NOPE:], gain[MLA_NOPE:])], axis=-1).astype(F32)

    def prep(p, rotary):
        bsz, tlen, _ = p.shape
        cq, ckv, k_rope = _split(p, (MLA_Q_RANK, MLA_KV_RANK, MLA_ROPE))
        q = (_rms(cq, q_norm_g) @ w_uq).reshape(bsz, tlen, MLA_HEADS, MLA_QK)
        kv = (_rms(ckv, kv_norm_g) @ w_ukv).reshape(bsz, tlen, MLA_HEADS, MLA_NOPE + MLA_V)
        k_rope = jnp.broadcast_to(k_rope[:, :, None, :], (bsz, tlen, MLA_HEADS, MLA_ROPE))
        k = jnp.concatenate([kv[..., :MLA_NOPE], k_rope], axis=-1)
        q, k = qk_norm(q, q_gain), qk_norm(k, k_gain)
        if rotary:
            q = jnp.concatenate([q[..., :MLA_NOPE], _rope(q[..., MLA_NOPE:], cos, sin)], axis=-1)
            k = jnp.concatenate([k[..., :MLA_NOPE], _rope(k[..., MLA_NOPE:], cos, sin)], axis=-1)
        return q, k, kv[..., MLA_NOPE:].astype(F32)

    qc, kc, vc = prep(p_ctx, False)
    ql, kl, vl = prep(p_lat, True)
    o_ctx = _attend(qc, kc, vc)
    k_all = jnp.concatenate([kc, kl], axis=1)
    v_all = jnp.concatenate([vc, vl], axis=1)
    bsz, tlen = ql.shape[:2]
    q_blocks = jnp.moveaxis(ql.reshape(bsz, tlen // Q_BLOCK, Q_BLOCK, MLA_HEADS, MLA_QK), 1, 0)
    o_lat = lax.map(lambda qb: _attend(qb, k_all, v_all), q_blocks)
    o_lat = jnp.moveaxis(o_lat, 0, 1).reshape(bsz, tlen, GROUP_W)
    return o_ctx.reshape(bsz, -1, GROUP_W).astype(dtype), o_lat.astype(dtype)


def _sq_relu_mlp(h, w1, w2):
    return jnp.square(jax.nn.relu(h @ w1)) @ w2


def setup_inputs(seed: int = 0):
    key = jax.random.key(seed)
    ks = iter(jax.random.split(key, 48))

    def nrm(shape, scale):
        return scale * jax.random.normal(next(ks), shape, F32)

    def gain(shape):
        return 1.0 + nrm(shape, 0.02)

    def unif(shape, lo, hi):
        return jax.random.uniform(next(ks), shape, F32, lo, hi)

    def dt_bias(shape):
        dt = jnp.exp(unif(shape, math.log(1e-3), math.log(1e-1)))
        return dt + jnp.log(-jnp.expm1(-dt))

    L = DEPTH
    return {
        'x': nrm((BATCH, SEQ, D_MODEL), 1.0),
        'c': nrm((BATCH, D_MODEL), 1.0),
        'ctx': nrm((BATCH, CTX_LEN, D_MODEL), 1.0),
        'c_ctx': nrm((D_MODEL,), 1.0),
        'w_mod': nrm((L, D_MODEL, 6 * D_MODEL), 0.5 * D_MODEL ** -0.5),
        'b_mod': nrm((L, 6 * D_MODEL), 0.02),
        'norm1_g': gain((L, D_MODEL)),
        'norm2_g': gain((L, D_MODEL)),
        'w_in': nrm((L, D_MODEL, IN_WIDTH), D_MODEL ** -0.5),
        'w_out': nrm((L, MIX_WIDTH, D_MODEL), MIX_WIDTH ** -0.5),
        'w_ff1': nrm((L, D_MODEL, D_FF), D_MODEL ** -0.5),
        'w_ff2': nrm((L, D_FF, D_MODEL), D_FF ** -0.5),
        'gdn_conv_w': nrm((L, CONV_K, 3 * GROUP_W), CONV_K ** -0.5),
        'gdn_a_log': jnp.log(unif((L, 2, GDN_HEADS), 1.0, 16.0)),
        'gdn_dt_bias': dt_bias((L, 2, GDN_HEADS)),
        'gdn_norm_g': gain((L, GDN_HEAD_DIM)),
        's5_a_re': -0.5 + nrm((L, 2, S5_GROUPS, S5_STATE), 0.01),
        's5_a_im': math.pi * jnp.arange(S5_STATE, dtype=F32) + nrm((L, 2, S5_GROUPS, S5_STATE), 0.01),
        's5_log_step': unif((L, 2, S5_GROUPS), math.log(1e-3), math.log(1e-1)),
        's5_b_re': nrm((L, 2, S5_GROUPS, S5_STATE, S5_GROUP), (2 * S5_GROUP) ** -0.5),
        's5_b_im': nrm((L, 2, S5_GROUPS, S5_STATE, S5_GROUP), (2 * S5_GROUP) ** -0.5),
        's5_c_re': nrm((L, 2, S5_GROUPS, S5_GROUP, S5_STATE), (2 * S5_STATE) ** -0.5),
        's5_c_im': nrm((L, 2, S5_GROUPS, S5_GROUP, S5_STATE), (2 * S5_STATE) ** -0.5),
        's5_d': nrm((L, GROUP_W), 0.5),
        's5_w_glu': nrm((L, GROUP_W, GROUP_W), GROUP_W ** -0.5),
        's5_b_glu': nrm((L, GROUP_W), 0.02),
        'ssd_conv_w': nrm((L, CONV_K, SSD_CONV_CH), CONV_K ** -0.5),
        'ssd_conv_b': nrm((L, SSD_CONV_CH), 0.02),
        'ssd_a_log': jnp.log(unif((L, 2, SSD_HEADS), 1.0, 16.0)),
        'ssd_dt_bias': dt_bias((L, 2, SSD_HEADS)),
        'ssd_d': gain((L, SSD_HEADS)),
        'ssd_norm_g': gain((L, GROUP_W)),
        'mla_q_norm_g': gain((L, MLA_Q_RANK)),
        'mla_kv_norm_g': gain((L, MLA_KV_RANK)),
        'mla_w_uq': nrm((L, MLA_Q_RANK, MLA_HEADS * MLA_QK), MLA_Q_RANK ** -0.5),
        'mla_w_ukv': nrm((L, MLA_KV_RANK, MLA_HEADS * (MLA_NOPE + MLA_V)), MLA_KV_RANK ** -0.5),
        'mla_q_gain': gain((L, MLA_QK)),
        'mla_k_gain': gain((L, MLA_QK)),
    }


def reference(x, c, ctx, c_ctx, w_mod, b_mod, norm1_g, norm2_g, w_in, w_out, w_ff1, w_ff2,
              gdn_conv_w, gdn_a_log, gdn_dt_bias, gdn_norm_g,
              s5_a_re, s5_a_im, s5_log_step, s5_b_re, s5_b_im, s5_c_re, s5_c_im, s5_d, s5_w_glu, s5_b_glu,
              ssd_conv_w, ssd_conv_b, ssd_a_log, ssd_dt_bias, ssd_d, ssd_norm_g,
              mla_q_norm_g, mla_kv_norm_g, mla_w_uq, mla_w_ukv, mla_q_gain, mla_k_gain):
    tlen = x.shape[1]
    rows = tlen // GRID_W
    cos, sin = _axial_rope_angles(rows)
    cond_lat = jax.nn.silu(c)[:, None, :]
    cond_ctx = jax.nn.silu(c_ctx)[None, None, :]
    h_lat, h_ctx = x, ctx
    for l in range(DEPTH):
        m_lat = jnp.split(cond_lat @ w_mod[l] + b_mod[l], 6, axis=-1)
        m_ctx = jnp.split(cond_ctx @ w_mod[l] + b_mod[l], 6, axis=-1)
        a_lat = _modulate(h_lat, norm1_g[l], m_lat[0], m_lat[1]) @ w_in[l]
        a_ctx = _modulate(h_ctx, norm1_g[l], m_ctx[0], m_ctx[1]) @ w_in[l]
        gdn_c, s5_c, ssd_c, mla_c = _split(a_ctx, IN_WIDTHS)
        gdn_l, s5_l, ssd_l, mla_l = _split(a_lat, IN_WIDTHS)
        ya = gdn_mixer(gdn_c, gdn_l, gdn_conv_w[l], gdn_a_log[l], gdn_dt_bias[l], gdn_norm_g[l])
        yb = s5_mixer(s5_c, s5_l, s5_a_re[l], s5_a_im[l], s5_log_step[l], s5_b_re[l], s5_b_im[l],
                      s5_c_re[l], s5_c_im[l], s5_d[l], s5_w_glu[l], s5_b_glu[l])
        yc = mamba2_mixer(ssd_c, ssd_l, ssd_conv_w[l], ssd_conv_b[l], ssd_a_log[l], ssd_dt_bias[l],
                          ssd_d[l], ssd_norm_g[l])
        yd = mla_mixer(mla_c, mla_l, mla_q_norm_g[l], mla_kv_norm_g[l], mla_w_uq[l], mla_w_ukv[l],
                       mla_q_gain[l], mla_k_gain[l], cos, sin)
        mix_lat = jnp.concatenate([ya[1], yb[1], yc[1], yd[1]], axis=-1)
        h_lat = h_lat + m_lat[2] * (mix_lat @ w_out[l])
        h_lat = h_lat + m_lat[5] * _sq_relu_mlp(_modulate(h_lat, norm2_g[l], m_lat[3], m_lat[4]),
                                                w_ff1[l], w_ff2[l])
        if l < DEPTH - 1:
            mix_ctx = jnp.concatenate([ya[0], yb[0], yc[0], yd[0]], axis=-1)
            h_ctx = h_ctx + m_ctx[2] * (mix_ctx @ w_out[l])
            h_ctx = h_ctx + m_ctx[5] * _sq_relu_mlp(_modulate(h_ctx, norm2_g[l], m_ctx[3], m_ctx[4]),
                                                    w_ff1[l], w_ff2[l])
    return h_lat
```

```python
import functools
import math

import jax
import jax.numpy as jnp
import numpy as np
from jax import lax
from jax.experimental import pallas as pl
from jax.experimental.pallas import tpu as pltpu

F32 = jnp.float32
BF16 = jnp.bfloat16

D_MODEL = 1024
DEPTH = 4
GRID_W = 64
GROUP_W = 256
D_FF = 4 * D_MODEL
CONV_K = 5
CHUNK = 64
EPS = 1e-6
ROPE_THETA = 10000.0

GDN_HEADS = 4
GDN_HEAD_DIM = 64
GDN_IN = 4 * GROUP_W + 4 * GDN_HEADS
S5_GROUP = 16
S5_GROUPS = 16
S5_STATE = 64
S5_SUB = 8
SSD_HEADS = 4
SSD_HEAD_DIM = 64
SSD_GROUPS = 2
SSD_STATE = 128
SSD_CONV_CH = GROUP_W + 2 * SSD_GROUPS * SSD_STATE
SSD_IN = GROUP_W + SSD_CONV_CH + 2 * SSD_HEADS
MLA_HEADS = 4
MLA_NOPE = 64
MLA_ROPE = 32
MLA_QK = MLA_NOPE + MLA_ROPE
MLA_V = 64
MLA_Q_RANK = 256
MLA_KV_RANK = 128
MLA_IN = MLA_Q_RANK + MLA_KV_RANK + MLA_ROPE
IN_WIDTH = GDN_IN + GROUP_W + SSD_IN + MLA_IN

LANES = 128
SUBLANES = 8
TB = 256
HALO = SUBLANES
VMEM_LIMIT = 56 * 1024 * 1024

C_GDN = 0
C_S5 = 1024
C_SSD = 1280
C_MLA = 2304
C_MISC = 2688
IN_COLS = 2944
MISC_A, MISC_B, MISC_DT, MISC_ROPE = 0, 8, 16, 64


def _mm(a, b):
    return jnp.dot(a.astype(BF16), b.astype(BF16), preferred_element_type=F32)


def _mm_inv(a, b):
    return jnp.dot(a, b, preferred_element_type=F32, precision=lax.Precision.HIGHEST)


def _mm_nt(a, b):
    return lax.dot_general(a.astype(BF16), b.astype(BF16), (((1,), (1,)), ((), ())),
                           preferred_element_type=F32)


def _mm_tn(a, b):
    return lax.dot_general(a.astype(BF16), b.astype(BF16), (((0,), (0,)), ((), ())),
                           preferred_element_type=F32)


def _mm_f32(a, b):
    return jnp.dot(a, b, preferred_element_type=F32, precision=lax.Precision.HIGHEST)


def _silu(x):
    return x * jax.nn.sigmoid(x)


def _softplus(x):
    return jnp.maximum(x, 0.0) + jnp.log1p(jnp.exp(-jnp.abs(x)))


def _rms_rows(x, g):
    return x * lax.rsqrt(jnp.mean(x * x, axis=-1, keepdims=True) + EPS) * g


def _blk_of(j, rev, nctx, nblk):
    if not rev:
        return j
    return jnp.where(j < nctx, nctx - 1 - j, nblk - 1 - (j - nctx))


def _chunk_masks(rev):
    row = lax.broadcasted_iota(jnp.int32, (TB, TB), 0)
    col = lax.broadcasted_iota(jnp.int32, (TB, TB), 1)
    same = (row // CHUNK) == (col // CHUNK)
    if rev:
        incl = same & (col >= row)
        strict = same & (col > row)
    else:
        incl = same & (col <= row)
        strict = same & (col < row)
    return same, incl, strict


def _conv_silu(cur, prev, nxt, ext_ref, w_ref, bias, lvalid, rvalid):
    ext_ref[0:HALO, :] = jnp.where(lvalid, prev, 0.0)
    ext_ref[HALO:HALO + TB, :] = cur
    ext_ref[HALO + TB:2 * HALO + TB, :] = jnp.where(rvalid, nxt, 0.0)
    acc = bias
    for k in range(CONV_K):
        term = w_ref[k:k + 1, :] * ext_ref[pl.ds(HALO - CONV_K // 2 + k, TB), :]
        acc = term if acc is None else acc + term
    return _silu(acc)


def _halo_valid(blk, nctx, nblk):
    lvalid = jnp.logical_and(blk != 0, blk != nctx)
    rvalid = jnp.logical_and(blk != nctx - 1, blk != nblk - 1)
    return lvalid, rvalid


def _col(x, lane):
    return x[:, lane:lane + 1]


def _mod_kernel(c_ref, w_ref, b_ref, o_ref):
    o_ref[0] = _mm(_silu(c_ref[...]), w_ref[0]) + b_ref[0]


def _modulation(cond, w_mod, b_mod):
    nl = w_mod.shape[0]
    tn = 512
    return pl.pallas_call(
        _mod_kernel,
        grid=(nl, 6 * D_MODEL // tn),
        in_specs=[pl.BlockSpec((16, D_MODEL), lambda l, n: (0, 0)),
                  pl.BlockSpec((1, D_MODEL, tn), lambda l, n: (l, 0, n)),
                  pl.BlockSpec((1, 1, tn), lambda l, n: (l, 0, n))],
        out_specs=pl.BlockSpec((1, 16, tn), lambda l, n: (l, 0, n)),
        out_shape=jax.ShapeDtypeStruct((nl, 16, 6 * D_MODEL), F32),
        name="modulation",
    )(cond, w_mod, b_mod.reshape(nl, 1, 6 * D_MODEL))


def _inproj_kernel(h_ref, mod_ref, g_ref, w_ref, o_gdn, o_s5, o_ssd, o_mla, o_misc):
    x = h_ref[0]
    shift = mod_ref[0, :, 0:D_MODEL]
    scale = mod_ref[0, :, D_MODEL:2 * D_MODEL]
    xm = (_rms_rows(x, g_ref[...]) * (1.0 + scale) + shift).astype(BF16)
    o_gdn[0] = jnp.dot(xm, w_ref[:, C_GDN:C_S5], preferred_element_type=F32)
    o_s5[0] = jnp.dot(xm, w_ref[:, C_S5:C_SSD], preferred_element_type=F32)
    o_ssd[0] = jnp.dot(xm, w_ref[:, C_SSD:C_MLA], preferred_element_type=F32)
    o_mla[0] = jnp.dot(xm, w_ref[:, C_MLA:C_MISC], preferred_element_type=F32)
    o_misc[0] = jnp.dot(xm, w_ref[:, C_MISC:IN_COLS], preferred_element_type=F32)


def _inproj(h, mod, g, w, nctx):
    bsz, ttot, _ = h.shape
    nblk = ttot // TB
    widths = (C_S5 - C_GDN, C_SSD - C_S5, C_MLA - C_SSD, C_MISC - C_MLA, IN_COLS - C_MISC)
    return pl.pallas_call(
        _inproj_kernel,
        grid=(bsz, nblk),
        in_specs=[pl.BlockSpec((1, TB, D_MODEL), lambda b, j: (b, j, 0)),
                  pl.BlockSpec((1, 1, 6 * D_MODEL), lambda b, j: (2 * b + (j >= nctx).astype(jnp.int32), 0, 0)),
                  pl.BlockSpec((1, D_MODEL), lambda b, j: (0, 0)),
                  pl.BlockSpec((D_MODEL, IN_COLS), lambda b, j: (0, 0))],
        out_specs=[pl.BlockSpec((1, TB, wd), lambda b, j: (b, j, 0)) for wd in widths],
        out_shape=[jax.ShapeDtypeStruct((bsz, ttot, wd), F32) for wd in widths],
        compiler_params=pltpu.CompilerParams(vmem_limit_bytes=VMEM_LIMIT),
        name="inproj",
    )(h, mod, g, w)


def _gdn_kernel(rev, nctx, nblk, *refs):
    if rev:
        (cur_ref, prev_ref, next_ref, misc_ref, convw_ref, alog_ref, dtb_ref, ofwd_ref, normg_ref,
         out_ref, ext_ref, s_ref) = refs
    else:
        (cur_ref, prev_ref, next_ref, misc_ref, convw_ref, alog_ref, dtb_ref,
         out_ref, ext_ref, s_ref) = refs
    j = pl.program_id(1)
    blk = _blk_of(j, rev, nctx, nblk)
    lvalid, rvalid = _halo_valid(blk, nctx, nblk)

    @pl.when(j == 0)
    def _():
        s_ref[...] = jnp.zeros_like(s_ref)

    nq = 3 * GROUP_W
    qkv = _conv_silu(cur_ref[0, :, 0:nq], prev_ref[0, :, 0:nq], next_ref[0, :, 0:nq], ext_ref, convw_ref,
                     None, lvalid, rvalid)
    same, incl, strict = _chunk_masks(rev)
    seg_ones = same[:, :].astype(F32)
    q, k, v = qkv[:, 0:GROUP_W], qkv[:, GROUP_W:2 * GROUP_W], qkv[:, 2 * GROUP_W:3 * GROUP_W]
    q = q * lax.rsqrt(_mm_f32(q * q, seg_ones) + EPS) * (GDN_HEAD_DIM ** -0.5)
    k = k * lax.rsqrt(_mm_f32(k * k, seg_ones) + EPS)

    misc = misc_ref[0]
    g_all = -jnp.exp(alog_ref[...]) * _softplus(misc + dtb_ref[...])
    beta_all = jax.nn.sigmoid(misc)
    cs = incl.astype(F32)
    gc_all = _mm_f32(cs, g_all)
    gcT_all = lax.dot_general(g_all, cs, (((0,), (1,)), ((), ())), preferred_element_type=F32,
                              precision=lax.Precision.HIGHEST)
    glast_all = _mm_f32(seg_ones, g_all)

    outs = []
    order = range(TB // CHUNK - 1, -1, -1) if rev else range(TB // CHUNK)
    for h in range(GDN_HEADS):
        lg = MISC_A + (GDN_HEADS if rev else 0) + h
        lb = MISC_B + (GDN_HEADS if rev else 0) + h
        hs = slice(h * GDN_HEAD_DIM, (h + 1) * GDN_HEAD_DIM)
        qh, kh, vh = q[:, hs], k[:, hs], v[:, hs]
        gc = _col(gc_all, lg)
        gl = _col(glast_all, lg)
        beta = _col(beta_all, lb)
        dmat = jnp.exp(jnp.where(incl, gc - gcT_all[lg:lg + 1, :], -jnp.inf))
        kb = kh * beta
        nmat = jnp.where(strict, -(_mm_nt(kb, kh) * dmat), 0.0)
        x = jnp.concatenate([vh * beta, kb * jnp.exp(gc)], axis=1)
        p = nmat
        for i in range(6):
            x = x + _mm_inv(p, x)
            if i < 5:
                p = _mm_inv(p, p)
        u, w = x[:, 0:GDN_HEAD_DIM], x[:, GDN_HEAD_DIM:]
        qk = jnp.where(incl, _mm_nt(qh, kh) * dmat, 0.0)
        qg = qh * jnp.exp(gc)
        kd = kh * jnp.exp(gl - gc)
        s = s_ref[h]
        o_parts = [None] * (TB // CHUNK)
        for c in order:
            r = slice(c * CHUNK, (c + 1) * CHUNK)
            v_new = u[r] - _mm(w[r], s)
            o_parts[c] = _mm(qg[r], s) + _mm(qk[r, r], v_new)
            s = s * jnp.exp(gl[c * CHUNK:c * CHUNK + 1, :]) + _mm_tn(kd[r], v_new)
        s_ref[h] = s
        outs.append(jnp.concatenate(o_parts, axis=0))
    o = jnp.concatenate(outs, axis=1)
    if rev:
        o = o + ofwd_ref[0]
        ms = _mm_f32(o * o, seg_ones) * (1.0 / GDN_HEAD_DIM)
        gate = cur_ref[0, :, nq:nq + GROUP_W]
        o = o * lax.rsqrt(ms + EPS) * normg_ref[...] * _silu(gate)
    out_ref[0] = o


def _gdn_pass(rev, a_gdn, a_misc, conv_w, alog, dtb, nctx, o_fwd=None, norm_g=None):
    bsz, ttot, wd = a_gdn.shape
    nblk = ttot // TB
    hb = TB // HALO
    nrow = ttot // HALO
    blk = lambda j: _blk_of(j, rev, nctx, nblk)
    in_specs = [pl.BlockSpec((1, TB, wd), lambda b, j: (b, blk(j), 0)),
                pl.BlockSpec((1, HALO, wd), lambda b, j: (b, jnp.maximum(blk(j) * hb - 1, 0), 0)),
                pl.BlockSpec((1, HALO, wd), lambda b, j: (b, jnp.minimum((blk(j) + 1) * hb, nrow - 1), 0)),
                pl.BlockSpec((1, TB, LANES), lambda b, j: (b, blk(j), 0)),
                pl.BlockSpec((SUBLANES, 3 * GROUP_W), lambda b, j: (0, 0)),
                pl.BlockSpec((1, LANES), lambda b, j: (0, 0)),
                pl.BlockSpec((1, LANES), lambda b, j: (0, 0))]
    args = [a_gdn, a_gdn, a_gdn, a_misc, conv_w, alog, dtb]
    if rev:
        in_specs += [pl.BlockSpec((1, TB, GROUP_W), lambda b, j: (b, blk(j), 0)),
                     pl.BlockSpec((1, GROUP_W), lambda b, j: (0, 0))]
        args += [o_fwd, norm_g]
    return pl.pallas_call(
        functools.partial(_gdn_kernel, rev, nctx, nblk),
        grid=(bsz, nblk),
        in_specs=in_specs,
        out_specs=pl.BlockSpec((1, TB, GROUP_W), lambda b, j: (b, blk(j), 0)),
        out_shape=jax.ShapeDtypeStruct((bsz, ttot, GROUP_W), F32),
        scratch_shapes=[pltpu.VMEM((TB + 2 * HALO, 3 * GROUP_W), F32),
                        pltpu.VMEM((GDN_HEADS, GDN_HEAD_DIM, GDN_HEAD_DIM), F32)],
        compiler_params=pltpu.CompilerParams(vmem_limit_bytes=VMEM_LIMIT),
        name="gdn_rev" if rev else "gdn_fwd",
    )(*args)


def _ssd_kernel(rev, nctx, nblk, *refs):
    if rev:
        (cur_ref, prev_ref, next_ref, misc_ref, convw_ref, convb_ref, avec_ref, dtb_ref, yfwd_ref, dskip_ref,
         normg_ref, out_ref, ext_ref, s_ref) = refs
    else:
        (cur_ref, prev_ref, next_ref, misc_ref, convw_ref, convb_ref, avec_ref, dtb_ref,
         out_ref, ext_ref, s_ref) = refs
    j = pl.program_id(1)
    blk = _blk_of(j, rev, nctx, nblk)
    lvalid, rvalid = _halo_valid(blk, nctx, nblk)

    @pl.when(j == 0)
    def _():
        s_ref[...] = jnp.zeros_like(s_ref)

    nc = SSD_CONV_CH
    xbc = _conv_silu(cur_ref[0, :, 0:nc], prev_ref[0, :, 0:nc], next_ref[0, :, 0:nc], ext_ref, convw_ref,
                     convb_ref[...], lvalid, rvalid)
    gw = SSD_GROUPS * SSD_STATE
    xs, bm, cm = xbc[:, 0:GROUP_W], xbc[:, GROUP_W:GROUP_W + gw], xbc[:, GROUP_W + gw:GROUP_W + 2 * gw]
    same, incl, _ = _chunk_masks(rev)
    seg_ones = same.astype(F32)
    cs = incl.astype(F32)
    dt_all = _softplus(misc_ref[0] + dtb_ref[...])
    la_all = dt_all * avec_ref[...]
    cum_all = _mm_f32(cs, la_all)
    cumT_all = lax.dot_general(la_all, cs, (((0,), (1,)), ((), ())), preferred_element_type=F32,
                               precision=lax.Precision.HIGHEST)
    clast_all = _mm_f32(seg_ones, la_all)

    cb = [_mm_nt(cm[:, g * SSD_STATE:(g + 1) * SSD_STATE], bm[:, g * SSD_STATE:(g + 1) * SSD_STATE])
          for g in range(SSD_GROUPS)]
    outs = []
    order = range(TB // CHUNK - 1, -1, -1) if rev else range(TB // CHUNK)
    for h in range(SSD_HEADS):
        ln = MISC_DT + (SSD_HEADS if rev else 0) + h
        g = h // (SSD_HEADS // SSD_GROUPS)
        gs = slice(g * SSD_STATE, (g + 1) * SSD_STATE)
        cum = _col(cum_all, ln)
        cl = _col(clast_all, ln)
        lmat = jnp.exp(jnp.where(incl, cum - cumT_all[ln:ln + 1, :], -jnp.inf))
        xdt = xs[:, h * SSD_HEAD_DIM:(h + 1) * SSD_HEAD_DIM] * _col(dt_all, ln)
        y = _mm(cb[g] * lmat, xdt)
        ce = cm[:, gs] * jnp.exp(cum)
        bw = bm[:, gs] * jnp.exp(cl - cum)
        s = s_ref[h]
        y_parts = [None] * (TB // CHUNK)
        for c in order:
            r = slice(c * CHUNK, (c + 1) * CHUNK)
            y_parts[c] = y[r] + _mm(ce[r], s)
            s = s * jnp.exp(cl[c * CHUNK:c * CHUNK + 1, :]) + _mm_tn(bw[r], xdt[r])
        s_ref[h] = s
        outs.append(jnp.concatenate(y_parts, axis=0))
    y = jnp.concatenate(outs, axis=1)
    if rev:
        z = cur_ref[0, :, nc:nc + GROUP_W]
        y = (y + yfwd_ref[0] + dskip_ref[...] * xs) * _silu(z)
        y = _rms_rows(y, normg_ref[...])
    out_ref[0] = y


def _ssd_pass(rev, a_ssd, a_misc, conv_w, conv_b, avec, dtb, nctx, y_fwd=None, dskip=None, norm_g=None):
    bsz, ttot, wd = a_ssd.shape
    nblk = ttot // TB
    hb = TB // HALO
    nrow = ttot // HALO
    blk = lambda j: _blk_of(j, rev, nctx, nblk)
    in_specs = [pl.BlockSpec((1, TB, wd), lambda b, j: (b, blk(j), 0)),
                pl.BlockSpec((1, HALO, wd), lambda b, j: (b, jnp.maximum(blk(j) * hb - 1, 0), 0)),
                pl.BlockSpec((1, HALO, wd), lambda b, j: (b, jnp.minimum((blk(j) + 1) * hb, nrow - 1), 0)),
                pl.BlockSpec((1, TB, LANES), lambda b, j: (b, blk(j), 0)),
                pl.BlockSpec((SUBLANES, SSD_CONV_CH), lambda b, j: (0, 0)),
                pl.BlockSpec((1, SSD_CONV_CH), lambda b, j: (0, 0)),
                pl.BlockSpec((1, LANES), lambda b, j: (0, 0)),
                pl.BlockSpec((1, LANES), lambda b, j: (0, 0))]
    args = [a_ssd, a_ssd, a_ssd, a_misc, conv_w, conv_b, avec, dtb]
    if rev:
        in_specs += [pl.BlockSpec((1, TB, GROUP_W), lambda b, j: (b, blk(j), 0)),
                     pl.BlockSpec((1, GROUP_W), lambda b, j: (0, 0)),
                     pl.BlockSpec((1, GROUP_W), lambda b, j: (0, 0))]
        args += [y_fwd, dskip, norm_g]
    return pl.pallas_call(
        functools.partial(_ssd_kernel, rev, nctx, nblk),
        grid=(bsz, nblk),
        in_specs=in_specs,
        out_specs=pl.BlockSpec((1, TB, GROUP_W), lambda b, j: (b, blk(j), 0)),
        out_shape=jax.ShapeDtypeStruct((bsz, ttot, GROUP_W), F32),
        scratch_shapes=[pltpu.VMEM((TB + 2 * HALO, SSD_CONV_CH), F32),
                        pltpu.VMEM((SSD_HEADS, SSD_STATE, SSD_HEAD_DIM), F32)],
        compiler_params=pltpu.CompilerParams(vmem_limit_bytes=VMEM_LIMIT),
        name="ssd_rev" if rev else "ssd_fwd",
    )(*args)


def _s5_kernel(rc, u_ref, m_ref, p_ref, q_ref, lam_ref, y_ref, sre_ref, sim_ref, hre_ref, him_ref):
    npair = u_ref.shape[1]
    nrow = u_ref.shape[2]
    pw = 2 * S5_STATE
    for p in range(npair):
        y_ref[0, p] = _mm(u_ref[0, p], m_ref[p])
    for d in range(2):
        for p in range(npair):
            s = _mm(u_ref[0, p], p_ref[d, p])
            sre_ref[:, p * pw:(p + 1) * pw] = s[:, 0:pw]
            sim_ref[:, p * pw:(p + 1) * pw] = s[:, pw:2 * pw]
        lre = lam_ref[d, 0]
        lim = lam_ref[d, 1]

        def step(i, carry):
            hr, hi = carry
            hre_ref[pl.ds(i, 1), :] = hr
            him_ref[pl.ds(i, 1), :] = hi
            sr = sre_ref[pl.ds(i, 1), :]
            si = sim_ref[pl.ds(i, 1), :]
            return hr * lre - hi * lim + sr, hr * lim + hi * lre + si

        zero = jnp.zeros((1, npair * pw), F32)
        if d == 0:
            lax.fori_loop(0, nrow, step, (zero, zero))
        else:
            carry = lax.fori_loop(0, rc, lambda t, c: step(rc - 1 - t, c), (zero, zero))
            lax.fori_loop(0, nrow - rc, lambda t, c: step(nrow - 1 - t, c), carry)
        for p in range(npair):
            hcat = jnp.concatenate([hre_ref[:, p * pw:(p + 1) * pw], him_ref[:, p * pw:(p + 1) * pw]], axis=1)
            y_ref[0, p] += _mm(hcat, q_ref[d, p])


def _s5_scan(u_t, m_mat, p_mat, q_mat, lam, rc):
    bsz, npair, nrow, wd = u_t.shape
    nstate = S5_GROUPS * S5_STATE
    return pl.pallas_call(
        functools.partial(_s5_kernel, rc),
        grid=(bsz,),
        in_specs=[pl.BlockSpec((1, npair, nrow, wd), lambda b: (b, 0, 0, 0)),
                  pl.BlockSpec((npair, wd, wd), lambda b: (0, 0, 0)),
                  pl.BlockSpec((2, npair, wd, wd), lambda b: (0, 0, 0, 0)),
                  pl.BlockSpec((2, npair, wd, wd), lambda b: (0, 0, 0, 0)),
                  pl.BlockSpec((2, 2, 1, nstate), lambda b: (0, 0, 0, 0))],
        out_specs=pl.BlockSpec((1, npair, nrow, wd), lambda b: (b, 0, 0, 0)),
        out_shape=jax.ShapeDtypeStruct((bsz, npair, nrow, wd), F32),
        scratch_shapes=[pltpu.VMEM((nrow, nstate), F32) for _ in range(4)],
        compiler_params=pltpu.CompilerParams(vmem_limit_bytes=VMEM_LIMIT),
        name="s5_scan",
    )(u_t, m_mat, p_mat, q_mat, lam)


def _s5_matrices(a_re, a_im, log_step, b_re, b_im, c_re, c_im):
    lam = lax.complex(a_re, a_im)
    dlam = lam * jnp.exp(log_step)[..., None]
    lam_bar = jnp.exp(dlam)
    b_bar = ((lam_bar - 1.0) / lam)[..., None] * lax.complex(b_re, b_im)
    c_mat = lax.complex(c_re, c_im)
    js = jnp.arange(S5_SUB + 1, dtype=F32)
    pw = jnp.exp(js[:, None, None, None] * dlam[None])
    kern = jnp.sum(c_mat[None, :, :, :, None, :] * pw[:S5_SUB, :, :, None, None, :]
                   * jnp.swapaxes(b_bar, -1, -2)[None, :, :, None, :, :], axis=-1).real
    tau = np.arange(S5_SUB)
    lag_f = tau[None, :] - tau[:, None]
    kf = jnp.where((lag_f >= 0)[:, :, None, None, None], kern[np.clip(lag_f, 0, S5_SUB - 1), 0], 0.0)
    kb = jnp.where((lag_f <= 0)[:, :, None, None, None], kern[np.clip(-lag_f, 0, S5_SUB - 1), 1], 0.0)
    m_g = jnp.transpose(kf + kb, (2, 0, 4, 1, 3))
    m_g = m_g.reshape(S5_GROUPS, LANES, LANES)
    pf = pw[S5_SUB - 1 - tau, 0][:, :, None, :] * jnp.swapaxes(b_bar[0], -1, -2)[None]
    pb = pw[tau, 1][:, :, None, :] * jnp.swapaxes(b_bar[1], -1, -2)[None]
    p_g = jnp.stack([pf, pb]).transpose(0, 2, 1, 3, 4).reshape(2, S5_GROUPS, LANES, S5_STATE)
    qf = c_mat[0][:, None, :, :] * pw[tau + 1, 0][:, :, None, :].transpose(1, 0, 2, 3)
    qb = c_mat[1][:, None, :, :] * pw[S5_SUB - tau, 1][:, :, None, :].transpose(1, 0, 2, 3)
    q_g = jnp.stack([qf, qb]).reshape(2, S5_GROUPS, LANES, S5_STATE).swapaxes(-1, -2)
    lam_l = pw[S5_SUB]

    npair = S5_GROUPS // 2
    z = jnp.zeros((npair, LANES, LANES), F32)

    def pair_diag(a, b):
        return jnp.concatenate([jnp.concatenate([a, z], axis=2), jnp.concatenate([z, b], axis=2)], axis=1)

    m_pair = pair_diag(m_g[0::2], m_g[1::2])
    zs = jnp.zeros((npair, LANES, S5_STATE), F32)
    p_pair, q_pair = [], []
    for d in range(2):
        pr, pi = p_g[d].real, p_g[d].imag
        top = jnp.concatenate([pr[0::2], zs, pi[0::2], zs], axis=2)
        bot = jnp.concatenate([zs, pr[1::2], zs, pi[1::2]], axis=2)
        p_pair.append(jnp.concatenate([top, bot], axis=1))
        qr, qi = q_g[d].real, -q_g[d].imag
        zq = jnp.zeros((npair, S5_STATE, LANES), F32)
        rows = [jnp.concatenate([qr[0::2], zq], axis=2), jnp.concatenate([zq, qr[1::2]], axis=2),
                jnp.concatenate([qi[0::2], zq], axis=2), jnp.concatenate([zq, qi[1::2]], axis=2)]
        q_pair.append(jnp.concatenate(rows, axis=1))
    lam = jnp.stack([lam_l.real, lam_l.imag], axis=1).reshape(2, 2, 1, S5_GROUPS * S5_STATE)
    return (m_pair.astype(BF16), jnp.stack(p_pair).astype(BF16), jnp.stack(q_pair).astype(BF16), lam)


def _s5_to_rows(u):
    bsz, ttot, _ = u.shape
    u = u.reshape(bsz, ttot // S5_SUB, S5_SUB, S5_GROUPS // 2, 2, S5_GROUP)
    return jnp.transpose(u, (0, 3, 1, 4, 2, 5)).reshape(bsz, S5_GROUPS // 2, ttot // S5_SUB, 2 * LANES)


def _s5_from_rows(y, ttot):
    bsz = y.shape[0]
    y = y.reshape(bsz, S5_GROUPS // 2, ttot // S5_SUB, 2, S5_SUB, S5_GROUP)
    return jnp.transpose(y, (0, 2, 4, 1, 3, 5)).reshape(bsz, ttot, GROUP_W)


def _mla_prep_kernel(a_ref, misc_ref, cos_ref, sin_ref, qg_ref, kvg_ref, wq_ref, wkv_ref, gains_ref, sel_ref,
                     q_out, k_out, v_out):
    a = a_ref[0]
    cq = _rms_rows(a[:, 0:MLA_Q_RANK], qg_ref[...])
    ckv = _rms_rows(a[:, MLA_Q_RANK:MLA_Q_RANK + MLA_KV_RANK], kvg_ref[...])
    qx = _mm(cq, wq_ref[...])
    kvx = _mm(ckv, wkv_ref[...])
    cosf, sinf = cos_ref[...], sin_ref[...]
    sel = sel_ref[...]
    lane = lax.broadcasted_iota(jnp.int32, (TB, LANES), 1)
    rope_lanes = (lane >= MLA_NOPE) & (lane < MLA_QK)
    misc = misc_ref[0]
    k_rope = jnp.where(rope_lanes, misc[:, 0:LANES], 0.0)
    k_rot = misc[:, LANES:2 * LANES]

    def norm_rope(xa, xb, ga, gb):
        inv = lax.rsqrt(_mm_f32(xa * xa, sel) + EPS)
        return (xa * inv * ga) * cosf + (xb * inv * gb) * sinf

    scale = MLA_QK ** -0.5
    for h in range(MLA_HEADS):
        qa = qx[:, h * 2 * LANES:h * 2 * LANES + LANES]
        qb = qx[:, h * 2 * LANES + LANES:(h + 1) * 2 * LANES]
        q_out[0, h] = (norm_rope(qa, qb, gains_ref[0:1, :], gains_ref[1:2, :]) * scale).astype(BF16)
        ka = kvx[:, h * LANES:(h + 1) * LANES] + k_rope
        k_out[0, h] = norm_rope(ka, k_rot, gains_ref[2:3, :], gains_ref[3:4, :]).astype(BF16)
    v_out[0] = kvx[:, MLA_HEADS * LANES:].astype(BF16)


def _mla_prep(a_mla, a_misc, cosf, sinf, qg, kvg, wq, wkv, gains, sel):
    bsz, ttot, wd = a_mla.shape
    nblk = ttot // TB
    const = lambda shape: pl.BlockSpec(shape, lambda b, j: tuple(0 for _ in shape))
    return pl.pallas_call(
        _mla_prep_kernel,
        grid=(bsz, nblk),
        in_specs=[pl.BlockSpec((1, TB, wd), lambda b, j: (b, j, 0)),
                  pl.BlockSpec((1, TB, 2 * LANES), lambda b, j: (b, j, 0)),
                  pl.BlockSpec((TB, LANES), lambda b, j: (j, 0)),
                  pl.BlockSpec((TB, LANES), lambda b, j: (j, 0)),
                  const((1, MLA_Q_RANK)), const((1, MLA_KV_RANK)),
                  const(wq.shape), const(wkv.shape), const((SUBLANES, LANES)), const((LANES, LANES))],
        out_specs=[pl.BlockSpec((1, MLA_HEADS, TB, LANES), lambda b, j: (b, 0, j, 0)),
                   pl.BlockSpec((1, MLA_HEADS, TB, LANES), lambda b, j: (b, 0, j, 0)),
                   pl.BlockSpec((1, TB, GROUP_W), lambda b, j: (b, j, 0))],
        out_shape=[jax.ShapeDtypeStruct((bsz, MLA_HEADS, ttot, LANES), BF16),
                   jax.ShapeDtypeStruct((bsz, MLA_HEADS, ttot, LANES), BF16),
                   jax.ShapeDtypeStruct((bsz, ttot, GROUP_W), BF16)],
        compiler_params=pltpu.CompilerParams(vmem_limit_bytes=VMEM_LIMIT),
        name="mla_prep",
    )(a_mla, a_misc, cosf, sinf, qg, kvg, wq, wkv, gains, sel)


def _attn_kernel(nctx, q_ref, k_ref, v_ref, o_ref):
    j = pl.program_id(1)

    def attend(nkeys):
        outs = []
        for h in range(MLA_HEADS):
            s = lax.dot_general(q_ref[0, h], k_ref[0, h, 0:nkeys, :], (((1,), (1,)), ((), ())),
                                preferred_element_type=F32)
            p = jnp.exp(s - jnp.max(s, axis=-1, keepdims=True))
            l = jnp.sum(p, axis=-1, keepdims=True)
            o = jnp.dot(p.astype(BF16), v_ref[0, 0:nkeys, h * MLA_V:(h + 1) * MLA_V], preferred_element_type=F32)
            outs.append(o / l)
        o_ref[0] = jnp.concatenate(outs, axis=1)

    @pl.when(j < nctx)
    def _():
        attend(nctx * TB)

    @pl.when(j >= nctx)
    def _():
        attend(k_ref.shape[2])


def _attention(q, k, v, nctx):
    bsz, nh, ttot, _ = q.shape
    nblk = ttot // TB
    return pl.pallas_call(
        functools.partial(_attn_kernel, nctx),
        grid=(bsz, nblk),
        in_specs=[pl.BlockSpec((1, nh, TB, LANES), lambda b, j: (b, 0, j, 0)),
                  pl.BlockSpec((1, nh, ttot, LANES), lambda b, j: (b, 0, 0, 0)),
                  pl.BlockSpec((1, ttot, GROUP_W), lambda b, j: (b, 0, 0))],
        out_specs=pl.BlockSpec((1, TB, GROUP_W), lambda b, j: (b, j, 0)),
        out_shape=jax.ShapeDtypeStruct((bsz, ttot, GROUP_W), F32),
        compiler_params=pltpu.CompilerParams(vmem_limit_bytes=VMEM_LIMIT),
        name="mla_attention",
    )(q, k, v)


def _outmlp_kernel(h_ref, mod_ref, gdn_ref, s5y_ref, s5u_ref, ssd_ref, mla_ref, s5d_ref, wglu_ref, bglu_ref,
                   wout_ref, g2_ref, w1_ref, w2_ref, o_ref):
    gw = GROUP_W
    y = jax.nn.gelu(s5y_ref[0] + s5d_ref[...] * s5u_ref[0], approximate=True)
    yb = y * jax.nn.sigmoid(_mm(y, wglu_ref[...]) + bglu_ref[...])
    mix = (_mm(gdn_ref[0], wout_ref[0:gw, :]) + _mm(yb, wout_ref[gw:2 * gw, :])
           + _mm(ssd_ref[0], wout_ref[2 * gw:3 * gw, :]) + _mm(mla_ref[0], wout_ref[3 * gw:4 * gw, :]))
    d = D_MODEL
    gate1 = mod_ref[0, :, 2 * d:3 * d]
    shift2 = mod_ref[0, :, 3 * d:4 * d]
    scale2 = mod_ref[0, :, 4 * d:5 * d]
    gate2 = mod_ref[0, :, 5 * d:6 * d]
    h1 = h_ref[0] + gate1 * mix
    xm = (_rms_rows(h1, g2_ref[...]) * (1.0 + scale2) + shift2).astype(BF16)
    f = jnp.dot(xm, w1_ref[...], preferred_element_type=F32)
    f = jnp.square(jnp.maximum(f, 0.0)).astype(BF16)
    o_ref[0] = h1 + gate2 * jnp.dot(f, w2_ref[...], preferred_element_type=F32)


def _outmlp(h, mod, gdn, s5y, s5u, ssd, mla, s5d, wglu, bglu, wout, g2, w1, w2, nctx):
    bsz, ttot, _ = h.shape
    nblk = ttot // TB
    tok = lambda wd: pl.BlockSpec((1, TB, wd), lambda b, j: (b, j, 0))
    const = lambda shape: pl.BlockSpec(shape, lambda b, j: tuple(0 for _ in shape), pipeline_mode=pl.Buffered(1))
    return pl.pallas_call(
        _outmlp_kernel,
        grid=(bsz, nblk),
        in_specs=[tok(D_MODEL),
                  pl.BlockSpec((1, 1, 6 * D_MODEL), lambda b, j: (2 * b + (j >= nctx).astype(jnp.int32), 0, 0)),
                  tok(GROUP_W), tok(GROUP_W), tok(GROUP_W), tok(GROUP_W), tok(GROUP_W),
                  const((1, GROUP_W)), const((GROUP_W, GROUP_W)), const((1, GROUP_W)),
                  const((D_MODEL, D_MODEL)), const((1, D_MODEL)),
                  const((D_MODEL, D_FF)), const((D_FF, D_MODEL))],
        out_specs=tok(D_MODEL),
        out_shape=jax.ShapeDtypeStruct((bsz, ttot, D_MODEL), F32),
        compiler_params=pltpu.CompilerParams(vmem_limit_bytes=VMEM_LIMIT),
        name="outproj_mlp",
    )(h, mod, gdn, s5y, s5u, ssd, mla, s5d, wglu, bglu, wout, g2, w1, w2)


def _deinterleave():
    half = MLA_ROPE // 2
    src = np.concatenate([2 * np.arange(half), 2 * np.arange(half) + 1])
    rot_src = np.concatenate([2 * np.arange(half) + 1, 2 * np.arange(half)])
    rot_sgn = np.concatenate([-np.ones(half), np.ones(half)]).astype(np.float32)
    return src, rot_src, rot_sgn


def _gather_cols(w, idx, sgn):
    idx = np.asarray(idx)
    cols = jnp.take(w, jnp.asarray(np.maximum(idx, 0)), axis=-1)
    return cols * jnp.asarray(np.where(idx >= 0, sgn, 0.0).astype(np.float32))


def _inproj_columns():
    src, rot_src, rot_sgn = _deinterleave()
    idx = -np.ones(IN_COLS, np.int64)
    sgn = np.ones(IN_COLS, np.float32)
    o_s5 = GDN_IN
    o_ssd = o_s5 + GROUP_W
    o_mla = o_ssd + SSD_IN
    idx[C_GDN:C_GDN + 4 * GROUP_W] = np.arange(4 * GROUP_W)
    idx[C_S5:C_S5 + GROUP_W] = o_s5 + np.arange(GROUP_W)
    idx[C_SSD:C_SSD + SSD_CONV_CH] = o_ssd + GROUP_W + np.arange(SSD_CONV_CH)
    idx[C_SSD + SSD_CONV_CH:C_SSD + SSD_CONV_CH + GROUP_W] = o_ssd + np.arange(GROUP_W)
    idx[C_MLA:C_MLA + MLA_Q_RANK + MLA_KV_RANK] = o_mla + np.arange(MLA_Q_RANK + MLA_KV_RANK)
    nab = 2 * GDN_HEADS
    idx[C_MISC + MISC_A:C_MISC + MISC_A + nab] = 4 * GROUP_W + np.arange(nab)
    idx[C_MISC + MISC_B:C_MISC + MISC_B + nab] = 4 * GROUP_W + nab + np.arange(nab)
    idx[C_MISC + MISC_DT:C_MISC + MISC_DT + 2 * SSD_HEADS] = o_ssd + GROUP_W + SSD_CONV_CH + np.arange(2 * SSD_HEADS)
    o_rope = o_mla + MLA_Q_RANK + MLA_KV_RANK
    idx[C_MISC + MISC_ROPE:C_MISC + MISC_ROPE + MLA_ROPE] = o_rope + src
    idx[C_MISC + LANES + MISC_ROPE:C_MISC + LANES + MISC_ROPE + MLA_ROPE] = o_rope + rot_src
    sgn[C_MISC + LANES + MISC_ROPE:C_MISC + LANES + MISC_ROPE + MLA_ROPE] = rot_sgn
    return idx, sgn


def _mla_columns():
    src, rot_src, rot_sgn = _deinterleave()
    qi = -np.ones(MLA_HEADS * 2 * LANES, np.int64)
    qs = np.ones(MLA_HEADS * 2 * LANES, np.float32)
    ki = -np.ones(MLA_HEADS * LANES + GROUP_W, np.int64)
    for h in range(MLA_HEADS):
        b = h * 2 * LANES
        qi[b:b + MLA_NOPE] = h * MLA_QK + np.arange(MLA_NOPE)
        qi[b + MLA_NOPE:b + MLA_QK] = h * MLA_QK + MLA_NOPE + src
        qi[b + LANES + MLA_NOPE:b + LANES + MLA_QK] = h * MLA_QK + MLA_NOPE + rot_src
        qs[b + LANES + MLA_NOPE:b + LANES + MLA_QK] = rot_sgn
        ki[h * LANES:h * LANES + MLA_NOPE] = h * (MLA_NOPE + MLA_V) + np.arange(MLA_NOPE)
        ki[MLA_HEADS * LANES + h * MLA_V:MLA_HEADS * LANES + (h + 1) * MLA_V] = (
            h * (MLA_NOPE + MLA_V) + MLA_NOPE + np.arange(MLA_V))
    gi = -np.ones((2, LANES), np.int64)
    gi[0, 0:MLA_NOPE] = np.arange(MLA_NOPE)
    gi[0, MLA_NOPE:MLA_QK] = MLA_NOPE + src
    gi[1, MLA_NOPE:MLA_QK] = MLA_NOPE + rot_src
    return qi, qs, ki, gi


def _rope_tables(tc, tl):
    pos = jnp.arange(tl)
    row = (pos // GRID_W).astype(F32)
    col = (pos % GRID_W).astype(F32)
    n_freq = MLA_ROPE // 4
    inv_freq = ROPE_THETA ** (-jnp.arange(n_freq, dtype=F32) / n_freq)
    ang = jnp.concatenate([row[:, None] * inv_freq, col[:, None] * inv_freq], axis=-1)
    ang = jnp.concatenate([jnp.zeros((tc, MLA_ROPE // 2), F32), ang], axis=0)
    ones = jnp.ones((tc + tl, MLA_NOPE), F32)
    zeros = jnp.zeros((tc + tl, LANES - MLA_QK), F32)
    cosf = jnp.concatenate([ones, jnp.cos(ang), jnp.cos(ang), zeros], axis=1)
    sinf = jnp.concatenate([0.0 * ones, jnp.sin(ang), jnp.sin(ang), zeros], axis=1)
    return cosf, sinf


def _lane_vec(vals, offset, width=LANES):
    vals = vals.reshape(-1)
    return jnp.zeros((1, width), F32).at[0, offset:offset + vals.shape[0]].set(vals)


def _pad_rows(w, rows):
    return jnp.concatenate([w, jnp.zeros((rows - w.shape[0],) + w.shape[1:], w.dtype)], axis=0)


def kernel(x, c, ctx, c_ctx, w_mod, b_mod, norm1_g, norm2_g, w_in, w_out, w_ff1, w_ff2, gdn_conv_w, gdn_a_log, gdn_dt_bias, gdn_norm_g, s5_a_re, s5_a_im, s5_log_step, s5_b_re, s5_b_im, s5_c_re, s5_c_im, s5_d, s5_w_glu, s5_b_glu, ssd_conv_w, ssd_conv_b, ssd_a_log, ssd_dt_bias, ssd_d, ssd_norm_g, mla_q_norm_g, mla_kv_norm_g, mla_w_uq, mla_w_ukv, mla_q_gain, mla_k_gain):
    bsz, tl, _ = x.shape
    tc = ctx.shape[1]
    depth = w_in.shape[0]
    assert tc % TB == 0 and tl % TB == 0 and bsz + 1 <= 16
    nctx = tc // TB
    ttot = tc + tl

    in_idx, in_sgn = _inproj_columns()
    w_in_p = _gather_cols(w_in, in_idx, in_sgn).astype(BF16)
    qi, qs, ki, gi = _mla_columns()
    wq_p = _gather_cols(mla_w_uq, qi, qs).astype(BF16)
    wkv_p = _gather_cols(mla_w_ukv, ki, np.ones(ki.shape, np.float32)).astype(BF16)
    ones2 = np.ones(gi.shape, np.float32)
    gains = jnp.concatenate([_gather_cols(mla_q_gain, gi, ones2),
                             _gather_cols(mla_k_gain, gi, ones2),
                             jnp.zeros((depth, SUBLANES - 4, LANES), F32)], axis=1)
    lane = np.arange(LANES)
    sel = (np.where((lane[:, None] < MLA_NOPE) & (lane[None, :] < MLA_NOPE), 1.0 / MLA_NOPE, 0.0)
           + np.where((lane[:, None] >= MLA_NOPE) & (lane[:, None] < MLA_QK)
                      & (lane[None, :] >= MLA_NOPE) & (lane[None, :] < MLA_QK), 1.0 / MLA_ROPE, 0.0))
    sel = jnp.asarray(sel.astype(np.float32))
    cosf, sinf = _rope_tables(tc, tl)
    w_out_b, w1_b, w2_b, wglu_b = (t.astype(BF16) for t in (w_out, w_ff1, w_ff2, s5_w_glu))

    cond = _pad_rows(jnp.concatenate([c, c_ctx[None, :]], axis=0), 16)
    mod_all = _modulation(cond, w_mod, b_mod)
    pick = np.stack([np.full(bsz, bsz), np.arange(bsz)], axis=1).reshape(-1)
    mod_all = mod_all[:, pick][:, :, None, :]

    h = jnp.concatenate([ctx, x], axis=1)
    for l in range(depth):
        mod = mod_all[l]
        a_gdn, a_s5, a_ssd, a_mla, a_misc = _inproj(h, mod, norm1_g[l][None, :], w_in_p[l], nctx)

        conv_w = _pad_rows(gdn_conv_w[l], SUBLANES)
        alog = _lane_vec(gdn_a_log[l], MISC_A)
        dtb = _lane_vec(gdn_dt_bias[l], MISC_A)
        o_f = _gdn_pass(False, a_gdn, a_misc, conv_w, alog, dtb, nctx)
        y_gdn = _gdn_pass(True, a_gdn, a_misc, conv_w, alog, dtb, nctx, o_f,
                          jnp.tile(gdn_norm_g[l], GDN_HEADS)[None, :])

        mats = _s5_matrices(s5_a_re[l], s5_a_im[l], s5_log_step[l], s5_b_re[l], s5_b_im[l], s5_c_re[l], s5_c_im[l])
        y_s5 = _s5_from_rows(_s5_scan(_s5_to_rows(a_s5), *mats, tc // S5_SUB), ttot)

        sconv_w = _pad_rows(ssd_conv_w[l], SUBLANES)
        avec = _lane_vec(-jnp.exp(ssd_a_log[l]), MISC_DT)
        sdtb = _lane_vec(ssd_dt_bias[l], MISC_DT)
        y_f = _ssd_pass(False, a_ssd, a_misc, sconv_w, ssd_conv_b[l][None, :], avec, sdtb, nctx)
        y_ssd = _ssd_pass(True, a_ssd, a_misc, sconv_w, ssd_conv_b[l][None, :], avec, sdtb, nctx, y_f,
                          jnp.repeat(ssd_d[l], SSD_HEAD_DIM)[None, :], ssd_norm_g[l][None, :])

        q, k, v = _mla_prep(a_mla, a_misc, cosf, sinf, mla_q_norm_g[l][None, :], mla_kv_norm_g[l][None, :],
                            wq_p[l], wkv_p[l], gains[l], sel)
        y_mla = _attention(q, k, v, nctx)

        h = _outmlp(h, mod, y_gdn, y_s5, a_s5, y_ssd, y_mla, s5_d[l][None, :], wglu_b[l], s5_b_glu[l][None, :],
                    w_out_b[l], norm2_g[l][None, :], w1_b[l], w2_b[l], nctx)
    return h[:, tc:, :]
```
